```python
import math, functools
import jax, jax.numpy as jnp
from jax import lax
import numpy as np

D_MODEL = 1024
BATCH = 4
SEQ = 4096
DEPTH = 4
DEC_BATCH = 32
DEC_SEQ = 4
PAST_LEN = 8192
PAGE_SIZE = 128

EPS = 1e-6
SSD_HEAD_DIM = 64
SSD_INNER = D_MODEL
SSD_HEADS = SSD_INNER // SSD_HEAD_DIM
SSD_GROUPS = 4
SSD_HPG = SSD_HEADS // SSD_GROUPS
SSD_STATE = 128
SSD_CONV = 4
SSD_CHUNK = 128
SSD_CONV_DIM = SSD_INNER + 2 * SSD_GROUPS * SSD_STATE
ATT_HEADS = 8
ATT_HEAD_DIM = 128
ATT_KV_HEADS = 2
ATT_GROUP = ATT_HEADS // ATT_KV_HEADS
IDX_HEADS = 8
IDX_DIM = 64
TOPK_MAX = 256
Q_BLOCK = 128
N_BUCKETS = 32
MAX_DISTANCE = 128
SC_WIDTH = D_MODEL
SC_KERNEL = 3
N_EXPERT_GROUPS = 4
EXPERTS_PER_GROUP = 8
N_EXPERTS = N_EXPERT_GROUPS * EXPERTS_PER_GROUP
EXPERT_FF = 256
TOP_K_EXPERTS = 2

IN_SIZES = (SSD_INNER, SSD_CONV_DIM, SSD_HEADS,
            ATT_HEADS * ATT_HEAD_DIM, ATT_KV_HEADS * ATT_HEAD_DIM, ATT_KV_HEADS * ATT_HEAD_DIM,
            IDX_HEADS * IDX_DIM, IDX_DIM, IDX_HEADS,
            SC_WIDTH, SC_WIDTH, SC_WIDTH,
            3 * D_MODEL)
N_IN = sum(IN_SIZES)

kernel_name = "hybrid_ssd_dsa_shortconv_hmoe_step"

F32 = jnp.float32


def rms_norm(x, w):
    xf = x.astype(F32)
    y = xf * lax.rsqrt(jnp.mean(xf * xf, axis=-1, keepdims=True) + EPS)
    return (y * w.astype(F32)).astype(x.dtype)


def ada_modulation(c, w, b):
    m = jax.nn.silu(c) @ w + b
    return jnp.split(m[:, None, :], 6, axis=-1)


def causal_dwconv(u, prev, w):
    k = w.shape[0]
    n = u.shape[1]
    full = jnp.concatenate([prev.astype(u.dtype), u], axis=1)
    out = sum(full[:, i:i + n] * w[i] for i in range(k))
    return out, full[:, full.shape[1] - (k - 1):]


def t5_bucket(dist):
    n = jnp.maximum(dist, 0)
    max_exact = N_BUCKETS // 2
    large = max_exact + (jnp.log(jnp.maximum(n, 1).astype(F32) / max_exact)
                         / math.log(MAX_DISTANCE / max_exact) * (N_BUCKETS - max_exact)).astype(jnp.int32)
    large = jnp.minimum(large, N_BUCKETS - 1)
    return jnp.where(n < max_exact, n, large)


def ssd_scan(x, dt, a, bm, cm, h0, chunk):
    b, l, g, j, p = x.shape
    n = bm.shape[-1]
    c = l // chunk
    dt = dt.astype(F32)
    adt = (dt * a).reshape(b, c, chunk, g, j)
    xdt = (x.astype(F32) * dt[..., None]).reshape(b, c, chunk, g, j, p)
    bm = bm.astype(F32).reshape(b, c, chunk, g, n)
    cm = cm.astype(F32).reshape(b, c, chunk, g, n)
    cs = jnp.cumsum(adt, axis=2)
    causal = jnp.tril(jnp.ones((chunk, chunk), dtype=bool))
    seg = cs[:, :, :, None] - cs[:, :, None, :]
    decay_ls = jnp.exp(jnp.where(causal[None, None, :, :, None, None], seg, -jnp.inf))
    cb = jnp.einsum('bclgn,bcsgn->bclsg', cm, bm)
    y_diag = jnp.einsum('bclsgj,bcsgjp->bclgjp', cb[..., None] * decay_ls, xdt)
    decay_to_end = jnp.exp(cs[:, :, -1:] - cs)
    chunk_states = jnp.einsum('bcsgn,bcsgj,bcsgjp->bcgjpn', bm, decay_to_end, xdt)
    chunk_decay = jnp.exp(cs[:, :, -1])

    def step(h, inp):
        st, dec = inp
        return h * dec[..., None, None] + st, h

    h_final, h_in = lax.scan(step, h0.astype(F32),
                             (jnp.moveaxis(chunk_states, 1, 0), jnp.moveaxis(chunk_decay, 1, 0)))
    h_in = jnp.moveaxis(h_in, 0, 1)
    y_off = jnp.einsum('bclgn,bcgjpn,bclgj->bclgjp', cm, h_in, jnp.exp(cs))
    return (y_diag + y_off).reshape(b, l, g, j, p), h_final


def ssd_mixer(z, xbc, dt_raw, conv_prev, h0, conv_w, conv_b, dt_bias, a_log, d_skip, norm_w):
    b, l, _ = z.shape
    xbc, conv_new = causal_dwconv(xbc, conv_prev, conv_w)
    xbc = jax.nn.silu(xbc + conv_b)
    xs, bm, cm = jnp.split(xbc, [SSD_INNER, SSD_INNER + SSD_GROUPS * SSD_STATE], axis=-1)
    dt = jax.nn.softplus(dt_raw.astype(F32) + dt_bias.astype(F32))
    a = -jnp.exp(a_log.astype(F32))
    chunk = SSD_CHUNK if l % SSD_CHUNK == 0 else l
    xs5 = xs.reshape(b, l, SSD_GROUPS, SSD_HPG, SSD_HEAD_DIM)
    y, h_final = ssd_scan(xs5, dt.reshape(b, l, SSD_GROUPS, SSD_HPG), a.reshape(SSD_GROUPS, SSD_HPG),
                          bm.reshape(b, l, SSD_GROUPS, SSD_STATE), cm.reshape(b, l, SSD_GROUPS, SSD_STATE),
                          h0.reshape(b, SSD_GROUPS, SSD_HPG, SSD_HEAD_DIM, SSD_STATE), chunk)
    y = y + d_skip.astype(F32).reshape(SSD_GROUPS, SSD_HPG, 1) * xs5.astype(F32)
    gy = (y.reshape(b, l, SSD_INNER) * jax.nn.silu(z.astype(F32))).reshape(b, l, SSD_GROUPS, -1)
    gy = gy * lax.rsqrt(jnp.mean(gy * gy, axis=-1, keepdims=True) + EPS)
    out = (gy.reshape(b, l, SSD_INNER) * norm_w.astype(F32)).astype(z.dtype)
    return out, conv_new, h_final.reshape(b, SSD_HEADS, SSD_HEAD_DIM, SSD_STATE).astype(z.dtype)


def indexer_topk(qi, wi, ki, q_pos, k_pos, topk):
    s = jax.nn.relu(jnp.einsum('bqhd,bld->bqlh', qi, ki).astype(F32) * IDX_DIM ** -0.5)
    score = jnp.einsum('bqlh,bqh->bql', s, wi.astype(F32)) * IDX_HEADS ** -0.5
    score = jnp.where(k_pos[None, None, :] <= q_pos[None, :, None], score, -jnp.inf)
    _, idx = lax.top_k(score, topk)
    return idx


def attend_selected(q, k_sel, v_sel, key_pos, q_pos, rel_bias):
    b, nq = q.shape[:2]
    nsel = k_sel.shape[2]
    qg = q.reshape(b, nq, ATT_KV_HEADS, ATT_GROUP, ATT_HEAD_DIM)
    logits = jnp.einsum('bqhgd,bqnhd->bqhgn', qg, k_sel).astype(F32) * ATT_HEAD_DIM ** -0.5
    dist = q_pos[None, :, None] - key_pos
    bias = rel_bias[t5_bucket(dist)].astype(F32)
    bias = bias.reshape(b, nq, nsel, ATT_KV_HEADS, ATT_GROUP).transpose(0, 1, 3, 4, 2)
    valid = (dist >= 0)[:, :, None, None, :]
    p = jax.nn.softmax(jnp.where(valid, logits + bias, -jnp.inf), axis=-1).astype(v_sel.dtype)
    out = jnp.einsum('bqhgn,bqnhd->bqhgd', p, v_sel)
    return out.reshape(b, nq, ATT_HEADS * ATT_HEAD_DIM)


def dsa_prompt(q, k, v, qi, ki, wi, rel_bias):
    b, s = q.shape[:2]
    topk = min(TOPK_MAX, s // 4)
    nb = s // Q_BLOCK
    k_pos = jnp.arange(s, dtype=jnp.int32)
    take = jax.vmap(lambda arr, ix: arr[ix])

    def blk(args):
        i, qb, qib, wib = args
        q_pos = i * Q_BLOCK + jnp.arange(Q_BLOCK, dtype=jnp.int32)
        idx = indexer_topk(qib, wib, ki, q_pos, k_pos, topk)
        return attend_selected(qb, take(k, idx), take(v, idx), idx, q_pos, rel_bias)

    to_blocks = lambda a: a.reshape(b, nb, Q_BLOCK, *a.shape[2:]).swapaxes(0, 1)
    out = lax.map(blk, (jnp.arange(nb, dtype=jnp.int32), to_blocks(q), to_blocks(qi), to_blocks(wi)))
    return out.swapaxes(0, 1).reshape(b, s, ATT_HEADS * ATT_HEAD_DIM)


def dsa_sample(q, k, v, qi, ki, wi, cache_k, cache_v, cache_kidx, page_table, rel_bias):
    b, n_new = q.shape[:2]
    past = page_table.shape[1] * PAGE_SIZE
    total = past + n_new
    topk = min(TOPK_MAX, total // 4)
    q_pos = past + jnp.arange(n_new, dtype=jnp.int32)
    k_pos = jnp.arange(total, dtype=jnp.int32)
    ki_past = cache_kidx[page_table].reshape(b, past, IDX_DIM).astype(ki.dtype)
    idx = indexer_topk(qi, wi, jnp.concatenate([ki_past, ki], axis=1), q_pos, k_pos, topk)
    in_past = (idx < past)[..., None, None]
    pidx = jnp.minimum(idx, past - 1)
    phys = jax.vmap(lambda pt, ix: pt[ix])(page_table, pidx // PAGE_SIZE)
    off = pidx % PAGE_SIZE
    nidx = jnp.clip(idx - past, 0, n_new - 1)
    take = jax.vmap(lambda arr, ix: arr[ix])
    k_sel = jnp.where(in_past, cache_k[phys, off].astype(k.dtype), take(k, nidx))
    v_sel = jnp.where(in_past, cache_v[phys, off].astype(v.dtype), take(v, nidx))
    return attend_selected(q, k_sel, v_sel, idx, q_pos, rel_bias)


def hier_moe(h, wg, we, w_gate, w_up, w_down):
    b, l, _ = h.shape
    pg = jax.nn.softmax((h @ wg).astype(F32), axis=-1)
    g_sel = jnp.argmax(pg, axis=-1)
    pg_sel = jnp.max(pg, axis=-1)
    fine = (h @ we).astype(F32).reshape(b, l, N_EXPERT_GROUPS, EXPERTS_PER_GROUP)
    gidx = jnp.broadcast_to(g_sel[..., None, None], (b, l, 1, EXPERTS_PER_GROUP))
    pf = jax.nn.softmax(jnp.take_along_axis(fine, gidx, axis=2)[:, :, 0], axis=-1)
    top_p, top_i = lax.top_k(pf, TOP_K_EXPERTS)
    wts = pg_sel[..., None] * top_p / jnp.sum(top_p, axis=-1, keepdims=True)
    expert_id = g_sel[..., None] * EXPERTS_PER_GROUP + top_i
    combine = jnp.sum(jax.nn.one_hot(expert_id, N_EXPERTS, dtype=F32) * wts[..., None], axis=-2).astype(h.dtype)
    gate = jnp.einsum('bld,edf->blef', h, w_gate)
    up = jnp.einsum('bld,edf->blef', h, w_up)
    act = jax.nn.silu(gate) * up * combine[..., None]
    return jnp.einsum('blef,efd->bld', act, w_down)


def parallel_mixer(h, lw, attend, ssm_conv_prev, ssm_h0, sc_conv_prev):
    b, l, _ = h.shape
    split_at = np.cumsum(IN_SIZES)[:-1].tolist()
    (z, xbc, dt_raw, q, k, v, qi, ki, wi, gb, gc, hv, gate_logits) = jnp.split(h @ lw['w_in'], split_at, axis=-1)
    y_a, ssm_conv_new, ssm_h = ssd_mixer(z, xbc, dt_raw, ssm_conv_prev, ssm_h0, lw['ssm_conv_w'], lw['ssm_conv_b'],
                                         lw['ssm_dt_bias'], lw['ssm_a_log'], lw['ssm_d'], lw['ssm_norm_w'])
    k = k.reshape(b, l, ATT_KV_HEADS, ATT_HEAD_DIM)
    v = v.reshape(b, l, ATT_KV_HEADS, ATT_HEAD_DIM)
    y_b = attend(q.reshape(b, l, ATT_HEADS, ATT_HEAD_DIM), k, v, qi.reshape(b, l, IDX_HEADS, IDX_DIM), ki, wi)
    u, sc_conv_new = causal_dwconv(gc * hv, sc_conv_prev, lw['sc_conv_w'])
    y_c = gb * u
    g_a, g_b, g_c = jnp.split(jax.nn.sigmoid(gate_logits), 3, axis=-1)
    merged = g_a * (y_a @ lw['w_out_ssm']) + g_b * (y_b @ lw['w_out_att']) + g_c * (y_c @ lw['w_out_sc'])
    return merged @ lw['w_o'], (k, v, ki, ssm_h, ssm_conv_new, sc_conv_new)


def layer_forward(x, c, lw, attend, ssm_conv_prev, ssm_h0, sc_conv_prev):
    shift1, scale1, gate1, shift2, scale2, gate2 = ada_modulation(c, lw['ada_w'], lw['ada_b'])
    h = rms_norm(x, lw['norm1_w']) * (1 + scale1) + shift1
    mix, new_state = parallel_mixer(h, lw, attend, ssm_conv_prev, ssm_h0, sc_conv_prev)
    x = x + gate1 * mix
    h = rms_norm(x, lw['norm2_w']) * (1 + scale2) + shift2
    x = x + gate2 * hier_moe(h, lw['router_group_w'], lw['router_expert_w'], lw['exp_w_gate'], lw['exp_w_up'], lw['exp_w_down'])
    return x, new_state


def setup_inputs(seed: int = 0) -> dict:
    key = jax.random.key(seed)
    ks = iter(jax.random.split(key, 40))
    nrm = lambda shape, scale: jax.random.normal(next(ks), shape, F32) * scale
    n_pages = PAST_LEN // PAGE_SIZE
    n_pool = (DEC_BATCH * n_pages * 5) // 4
    page_table = jax.random.permutation(next(ks), n_pool)[:DEC_BATCH * n_pages].reshape(DEC_BATCH, n_pages).astype(jnp.int32)
    dt0 = jnp.exp(jax.random.uniform(next(ks), (DEPTH, SSD_HEADS)) * (math.log(0.1) - math.log(1e-3)) + math.log(1e-3))
    return {
        'x_prompt': nrm((BATCH, SEQ, D_MODEL), 1.0),
        'x_sample': nrm((DEC_BATCH, DEC_SEQ, D_MODEL), 1.0),
        'c_prompt': nrm((BATCH, D_MODEL), 1.0),
        'c_sample': nrm((DEC_BATCH, D_MODEL), 1.0),
        'cache_k': nrm((DEPTH, n_pool, PAGE_SIZE, ATT_KV_HEADS, ATT_HEAD_DIM), 1.0),
        'cache_v': nrm((DEPTH, n_pool, PAGE_SIZE, ATT_KV_HEADS, ATT_HEAD_DIM), 1.0),
        'cache_kidx': nrm((DEPTH, n_pool, PAGE_SIZE, IDX_DIM), 1.0),
        'state_ssm': nrm((DEPTH, DEC_BATCH, SSD_HEADS, SSD_HEAD_DIM, SSD_STATE), 0.3),
        'state_ssm_conv': nrm((DEPTH, DEC_BATCH, SSD_CONV - 1, SSD_CONV_DIM), 1.0),
        'state_sc_conv': nrm((DEPTH, DEC_BATCH, SC_KERNEL - 1, SC_WIDTH), 1.0),
        'page_table': page_table,
        'ada_w': nrm((DEPTH, D_MODEL, 6 * D_MODEL), 0.5 * D_MODEL ** -0.5),
        'ada_b': nrm((DEPTH, 6 * D_MODEL), 0.01),
        'norm1_w': 1.0 + nrm((DEPTH, D_MODEL), 0.01),
        'norm2_w': 1.0 + nrm((DEPTH, D_MODEL), 0.01),
        'w_in': nrm((DEPTH, D_MODEL, N_IN), D_MODEL ** -0.5),
        'ssm_conv_w': nrm((DEPTH, SSD_CONV, SSD_CONV_DIM), SSD_CONV ** -0.5),
        'ssm_conv_b': nrm((DEPTH, SSD_CONV_DIM), 0.01),
        'ssm_dt_bias': dt0 + jnp.log(-jnp.expm1(-dt0)),
        'ssm_a_log': jnp.log(jax.random.uniform(next(ks), (DEPTH, SSD_HEADS), minval=1.0, maxval=16.0)),
        'ssm_d': 1.0 + nrm((DEPTH, SSD_HEADS), 0.1),
        'ssm_norm_w': 1.0 + nrm((DEPTH, SSD_INNER), 0.01),
        'sc_conv_w': nrm((DEPTH, SC_KERNEL, SC_WIDTH), SC_KERNEL ** -0.5),
        'rel_bias': nrm((N_BUCKETS, ATT_HEADS), 0.5),
        'w_out_ssm': nrm((DEPTH, SSD_INNER, D_MODEL), SSD_INNER ** -0.5),
        'w_out_att': nrm((DEPTH, ATT_HEADS * ATT_HEAD_DIM, D_MODEL), (ATT_HEADS * ATT_HEAD_DIM) ** -0.5),
        'w_out_sc': nrm((DEPTH, SC_WIDTH, D_MODEL), SC_WIDTH ** -0.5),
        'w_o': nrm((DEPTH, D_MODEL, D_MODEL), D_MODEL ** -0.5),
        'router_group_w': nrm((DEPTH, D_MODEL, N_EXPERT_GROUPS), D_MODEL ** -0.5),
        'router_expert_w': nrm((DEPTH, D_MODEL, N_EXPERTS), D_MODEL ** -0.5),
        'exp_w_gate': nrm((DEPTH, N_EXPERTS, D_MODEL, EXPERT_FF), D_MODEL ** -0.5),
        'exp_w_up': nrm((DEPTH, N_EXPERTS, D_MODEL, EXPERT_FF), D_MODEL ** -0.5),
        'exp_w_down': nrm((DEPTH, N_EXPERTS, EXPERT_FF, D_MODEL), EXPERT_FF ** -0.5),
        'final_norm_w': 1.0 + nrm((D_MODEL,), 0.01),
    }


def reference(x_prompt, x_sample, c_prompt, c_sample, cache_k, cache_v, cache_kidx, state_ssm, state_ssm_conv,
              state_sc_conv, page_table, ada_w, ada_b, norm1_w, norm2_w, w_in, ssm_conv_w, ssm_conv_b, ssm_dt_bias,
              ssm_a_log, ssm_d, ssm_norm_w, sc_conv_w, rel_bias, w_out_ssm, w_out_att, w_out_sc, w_o,
              router_group_w, router_expert_w, exp_w_gate, exp_w_up, exp_w_down, final_norm_w):
    xp, xs = x_prompt, x_sample
    bp = xp.shape[0]
    prompt_rows = [[] for _ in range(6)]
    sample_rows = [[] for _ in range(6)]
    prompt_attn = functools.partial(dsa_prompt, rel_bias=rel_bias)
    for l in range(DEPTH):
        lw = {'ada_w': ada_w[l], 'ada_b': ada_b[l], 'norm1_w': norm1_w[l], 'norm2_w': norm2_w[l], 'w_in': w_in[l],
              'ssm_conv_w': ssm_conv_w[l], 'ssm_conv_b': ssm_conv_b[l], 'ssm_dt_bias': ssm_dt_bias[l],
              'ssm_a_log': ssm_a_log[l], 'ssm_d': ssm_d[l], 'ssm_norm_w': ssm_norm_w[l], 'sc_conv_w': sc_conv_w[l],
              'w_out_ssm': w_out_ssm[l], 'w_out_att': w_out_att[l], 'w_out_sc': w_out_sc[l], 'w_o': w_o[l],
              'router_group_w': router_group_w[l], 'router_expert_w': router_expert_w[l],
              'exp_w_gate': exp_w_gate[l], 'exp_w_up': exp_w_up[l], 'exp_w_down': exp_w_down[l]}
        xp, st_p = layer_forward(
            xp, c_prompt, lw, prompt_attn,
            jnp.zeros((bp, SSD_CONV - 1, SSD_CONV_DIM), xp.dtype),
            jnp.zeros((bp, SSD_HEADS, SSD_HEAD_DIM, SSD_STATE), xp.dtype),
            jnp.zeros((bp, SC_KERNEL - 1, SC_WIDTH), xp.dtype))
        sample_attn = functools.partial(dsa_sample, cache_k=cache_k[l], cache_v=cache_v[l], cache_kidx=cache_kidx[l],
                                        page_table=page_table, rel_bias=rel_bias)
        xs, st_s = layer_forward(xs, c_sample, lw, sample_attn, state_ssm_conv[l], state_ssm[l], state_sc_conv[l])
        for i in range(6):
            prompt_rows[i].append(st_p[i])
            sample_rows[i].append(st_s[i])
    y_prompt = rms_norm(xp, final_norm_w)
    y_sample = rms_norm(xs, final_norm_w)
    k_p, v_p, kidx_p, ssm_p, ssm_conv_p, sc_conv_p = [jnp.stack(r) for r in prompt_rows]
    k_s, v_s, kidx_s, ssm_s, ssm_conv_s, sc_conv_s = [jnp.stack(r) for r in sample_rows]
    return (y_prompt, y_sample, k_p, v_p, kidx_p, ssm_p, ssm_conv_p, sc_conv_p,
            k_s, v_s, kidx_s, ssm_s, ssm_conv_s, sc_conv_s)
```

```python
import functools
import math

import numpy as np
import jax
import jax.numpy as jnp
from jax import lax
from jax.experimental import pallas as pl
from jax.experimental.pallas import tpu as pltpu

F32 = jnp.float32
BF16 = jnp.bfloat16
I32 = jnp.int32

D_MODEL = 1024
DEPTH = 4
PAGE_SIZE = 128
EPS = 1e-6
SSD_HEAD_DIM = 64
SSD_INNER = D_MODEL
SSD_HEADS = 16
SSD_GROUPS = 4
SSD_HPG = 4
SSD_STATE = 128
SSD_CONV = 4
SSD_CHUNK = 128
SSD_CONV_DIM = SSD_INNER + 2 * SSD_GROUPS * SSD_STATE
ATT_HEADS = 8
ATT_HEAD_DIM = 128
ATT_KV_HEADS = 2
ATT_GROUP = 4
IDX_HEADS = 8
IDX_DIM = 64
TOPK_MAX = 256
Q_BLOCK = 128
N_BUCKETS = 32
MAX_DISTANCE = 128
SC_KERNEL = 3
N_EXPERT_GROUPS = 4
EXPERTS_PER_GROUP = 8
N_EXPERTS = 32
EXPERT_FF = 256

LANES = 128
SUBLANES = 8
VMEM_LIMIT = 56 * 1024 * 1024

_ORIG = dict(z=(0, 1024), xbc=(1024, 2048), dt=(3072, 16), q=(3088, 1024), k=(4112, 256), v=(4368, 256),
             qi=(4624, 512), ki=(5136, 64), wi=(5200, 8), gb=(5208, 1024), gc=(6232, 1024), hv=(7256, 1024),
             gates=(8280, 3072))
_ORDER = ("z", "xbc", "q", "k", "v", "qi", "gb", "gc", "hv", "gates", "dt", "ki", "wi")
COL = {}
_off = 0
for _n in _ORDER:
    COL[_n] = _off
    _off += -(-_ORIG[_n][1] // LANES) * LANES
NP_IN = _off
IN_TN = 1664
assert NP_IN % IN_TN == 0

NEG_BIG = -1e30
KEY_NEG_INF = -2139095041
INT_MIN = -2147483648


def _cparams(sem):
    return pltpu.CompilerParams(dimension_semantics=sem, vmem_limit_bytes=VMEM_LIMIT)


def _silu(x):
    return x * (1.0 / (1.0 + jnp.exp(-x)))


def _dot(a, b):
    return jnp.dot(a.astype(BF16), b.astype(BF16), preferred_element_type=F32)


def _dot_nt(a, b):
    return lax.dot_general(a.astype(BF16), b.astype(BF16), (((1,), (1,)), ((), ())), preferred_element_type=F32)


def _dot_exact(a, b):
    return jnp.dot(a, b, precision=lax.Precision.HIGHEST, preferred_element_type=F32)


def _ada_kernel(c_ref, w_ref, b_ref, o_ref):
    s = _silu(c_ref[...])
    o_ref[...] = _dot(s, w_ref[...]) + b_ref[...]


def ada_modulation_all(c_all, ada_w, ada_b):
    r = c_all.shape[0]
    tn = 1536
    return pl.pallas_call(
        _ada_kernel,
        grid=(DEPTH, 6 * D_MODEL // tn),
        in_specs=[pl.BlockSpec((r, D_MODEL), lambda l, j: (0, 0)),
                  pl.BlockSpec((None, D_MODEL, tn), lambda l, j: (l, 0, j)),
                  pl.BlockSpec((None, 1, tn), lambda l, j: (l, 0, j))],
        out_specs=pl.BlockSpec((None, r, tn), lambda l, j: (l, 0, j)),
        out_shape=jax.ShapeDtypeStruct((DEPTH, r, 6 * D_MODEL), F32),
        compiler_params=_cparams(("arbitrary", "arbitrary")),
        name="ada_modulation",
    )(c_all, ada_w, ada_b.reshape(DEPTH, 1, 6 * D_MODEL))


def _modulated_norm(x, nw, sc, sh):
    ms = jnp.mean(x * x, axis=-1, keepdims=True)
    y = x * lax.rsqrt(ms + EPS) * nw
    return y * (1.0 + sc) + sh


def _inproj_kernel(x_ref, sc_ref, sh_ref, nw_ref, w_ref, o_ref, h_ref):
    @pl.when(pl.program_id(1) == 0)
    def _():
        h_ref[...] = _modulated_norm(x_ref[...], nw_ref[...], sc_ref[...], sh_ref[...]).astype(BF16)

    o_ref[...] = jnp.dot(h_ref[...], w_ref[...], preferred_element_type=F32)


def _mod_spec(per_row, tm, rows_per_seq):
    if per_row:
        return pl.BlockSpec((tm, D_MODEL), lambda i, j: (i, 0))
    tiles = rows_per_seq // tm
    return pl.BlockSpec((None, 1, D_MODEL), lambda i, j: (i // tiles, 0, 0))


def in_projection(x2d, sc, sh, nw, w_bf, tm, per_row, rows_per_seq):
    t = x2d.shape[0]
    ms = _mod_spec(per_row, tm, rows_per_seq)
    return pl.pallas_call(
        _inproj_kernel,
        grid=(t // tm, NP_IN // IN_TN),
        in_specs=[pl.BlockSpec((tm, D_MODEL), lambda i, j: (i, 0)), ms, ms,
                  pl.BlockSpec((1, D_MODEL), lambda i, j: (0, 0)),
                  pl.BlockSpec((D_MODEL, IN_TN), lambda i, j: (0, j))],
        out_specs=pl.BlockSpec((tm, IN_TN), lambda i, j: (i, j)),
        out_shape=jax.ShapeDtypeStruct((t, NP_IN), F32),
        scratch_shapes=[pltpu.VMEM((tm, D_MODEL), BF16)],
        compiler_params=_cparams(("arbitrary", "arbitrary")),
        name="in_projection",
    )(x2d, sc, sh, nw, w_bf)


def _ssd_kernel(lv_last, z_ref, xs_ref, bm_ref, cm_ref, dt_ref, cprev_ref, h0_ref, cw_ref, cb_ref, dtb_ref,
                alog_ref, dsk_ref, nw_ref, y_ref, hfin_ref, cnew_ref, cbuf, hs):
    c = pl.program_id(1)
    nc = pl.num_programs(1)
    ch = SSD_CHUNK
    hd = SSD_HEAD_DIM

    @pl.when(c == 0)
    def _():
        cbuf[0:SUBLANES, :] = jnp.zeros((SUBLANES, SSD_CONV_DIM), F32)
        cbuf[SUBLANES - 3:SUBLANES, :] = cprev_ref[...]
        hs[...] = h0_ref[...]

    cbuf[SUBLANES:SUBLANES + ch, 0:SSD_INNER] = xs_ref[...]
    cbuf[SUBLANES:SUBLANES + ch, SSD_INNER:SSD_INNER + 512] = bm_ref[...]
    cbuf[SUBLANES:SUBLANES + ch, SSD_INNER + 512:SSD_CONV_DIM] = cm_ref[...]
    conv = cb_ref[...] + cbuf[pl.ds(SUBLANES - 3, ch), :] * cw_ref[0:1, :]
    for i in range(1, SSD_CONV):
        conv = conv + cbuf[pl.ds(SUBLANES - 3 + i, ch), :] * cw_ref[i:i + 1, :]
    xbc = _silu(conv)
    xs = xbc[:, 0:SSD_INNER]
    bm = xbc[:, SSD_INNER:SSD_INNER + 512]
    cm = xbc[:, SSD_INNER + 512:SSD_CONV_DIM]

    @pl.when(c == nc - 1)
    def _():
        cnew_ref[...] = cbuf[pl.ds(SUBLANES + lv_last - 3, 3), :]

    cbuf[0:SUBLANES, :] = cbuf[ch:ch + SUBLANES, :]

    row = lax.broadcasted_iota(I32, (ch, LANES), 0)
    col = lax.broadcasted_iota(I32, (ch, LANES), 1)
    x = dt_ref[...] + dtb_ref[...]
    dt = jnp.maximum(x, 0.0) + jnp.log(1.0 + jnp.exp(-jnp.abs(x)))
    if lv_last < ch:
        dt = jnp.where(row < lv_last, dt, 0.0)
    a = -jnp.exp(alog_ref[...])
    adt = dt * a
    tril = row >= col
    cs = _dot_exact(tril.astype(F32), adt)
    cst = cs.T
    cs_last = cs[ch - 1:ch, :]
    er = lax.broadcasted_iota(I32, (LANES, SSD_INNER), 0)
    ec = lax.broadcasted_iota(I32, (LANES, SSD_INNER), 1)
    expand = (ec // hd == er).astype(F32)
    dt_e = _dot_exact(dt, expand)
    ecs_e = _dot_exact(jnp.exp(cs), expand)
    dte_e = _dot_exact(jnp.exp(cs_last - cs), expand)
    xdt = xs * dt_e
    lane = lax.broadcasted_iota(I32, (ch, LANES), 1)

    ys = []
    for g in range(SSD_GROUPS):
        cm_g = cm[:, g * SSD_STATE:(g + 1) * SSD_STATE]
        bm_g = bm[:, g * SSD_STATE:(g + 1) * SSD_STATE]
        cb_g = _dot_nt(cm_g, bm_g)
        gs = slice(g * SSD_HPG * hd, (g + 1) * SSD_HPG * hd)
        y_g = _dot_nt(cm_g, hs[gs, :]) * ecs_e[:, gs]
        pieces = []
        for jp in range(SSD_HPG // 2):
            xpair = xdt[:, gs][:, jp * LANES:(jp + 1) * LANES]
            acc = None
            for half in range(2):
                h = g * SSD_HPG + jp * 2 + half
                seg = cs[:, h:h + 1] - cst[h:h + 1, :]
                dec = jnp.exp(jnp.where(tril, seg, -jnp.inf))
                m = cb_g * dec
                xh = jnp.where((lane < hd) if half == 0 else (lane >= hd), xpair, 0.0)
                t = _dot(m, xh)
                acc = t if acc is None else acc + t
            pieces.append(acc)
        y_g = y_g + jnp.concatenate(pieces, axis=1)
        ys.append(y_g)
        xw_t = (xdt[:, gs] * dte_e[:, gs]).T
        dec_t = ecs_e[:, gs].T[:, ch - 1:ch]
        hs[gs, :] = hs[gs, :] * dec_t + _dot(xw_t, bm_g)
    y = jnp.concatenate(ys, axis=1) + dsk_ref[...] * xs
    gy = y * _silu(z_ref[...])
    outs = []
    gw = SSD_INNER // SSD_GROUPS
    for g in range(SSD_GROUPS):
        t = gy[:, g * gw:(g + 1) * gw]
        ms = jnp.mean(t * t, axis=-1, keepdims=True)
        outs.append(t * lax.rsqrt(ms + EPS))
    y_ref[...] = (jnp.concatenate(outs, axis=1) * nw_ref[...]).astype(y_ref.dtype)

    @pl.when(c == nc - 1)
    def _():
        hfin_ref[...] = hs[...]


def ssd_mixer(proj3, conv_prev, h0, lw, lv_last):
    b, l, _ = proj3.shape
    nc = l // SSD_CHUNK
    ch = SSD_CHUNK

    def colspec(name, width, extra=0):
        blk = (COL[name] + extra) // width
        return pl.BlockSpec((None, ch, width), lambda i, c, blk=blk: (i, c, blk))

    full = lambda shape: pl.BlockSpec(shape, lambda i, c: (0,) * len(shape))
    perb = lambda shape: pl.BlockSpec((None,) + shape, lambda i, c: (i,) + (0,) * len(shape))
    return pl.pallas_call(
        functools.partial(_ssd_kernel, lv_last),
        grid=(b, nc),
        in_specs=[colspec("z", 1024), colspec("xbc", 1024), colspec("xbc", 512, 1024), colspec("xbc", 512, 1536),
                  colspec("dt", LANES), perb((SSD_CONV - 1, SSD_CONV_DIM)), perb((SSD_INNER, SSD_STATE)),
                  full((SSD_CONV, SSD_CONV_DIM)), full((1, SSD_CONV_DIM)), full((1, LANES)), full((1, LANES)),
                  full((1, SSD_INNER)), full((1, SSD_INNER))],
        out_specs=[pl.BlockSpec((None, ch, SSD_INNER), lambda i, c: (i, c, 0)),
                   perb((SSD_INNER, SSD_STATE)), perb((SSD_CONV - 1, SSD_CONV_DIM))],
        out_shape=[jax.ShapeDtypeStruct((b, l, SSD_INNER), BF16),
                   jax.ShapeDtypeStruct((b, SSD_INNER, SSD_STATE), F32),
                   jax.ShapeDtypeStruct((b, SSD_CONV - 1, SSD_CONV_DIM), F32)],
        scratch_shapes=[pltpu.VMEM((ch + SUBLANES, SSD_CONV_DIM), F32), pltpu.VMEM((SSD_INNER, SSD_STATE), F32)],
        compiler_params=_cparams(("arbitrary", "arbitrary")),
        name="ssd_mixer",
    )(proj3, proj3, proj3, proj3, proj3, conv_prev, h0, lw["ssm_conv_w"], lw["ssm_conv_b"], lw["dt_bias"],
      lw["a_log"], lw["d_skip"], lw["ssm_norm_w"])


def _bias_table_kernel(rb_ref, bk_ref, o_ref):
    h = pl.program_id(1)
    bk = bk_ref[...]
    val = jnp.full(bk.shape, rb_ref[N_BUCKETS - 1, h], F32)
    for b in range(N_BUCKETS - 2, -1, -1):
        val = jnp.where(bk == b, rb_ref[b, h], val)
    o_ref[...] = val


def bias_tables(rel_bias, buckets):
    n, r, _ = buckets.shape
    return pl.pallas_call(
        _bias_table_kernel,
        grid=(n, ATT_HEADS),
        in_specs=[pl.BlockSpec(memory_space=pltpu.SMEM),
                  pl.BlockSpec((None, r, LANES), lambda i, h: (i, 0, 0))],
        out_specs=pl.BlockSpec((None, None, r, LANES), lambda i, h: (i, h, 0, 0)),
        out_shape=jax.ShapeDtypeStruct((n, ATT_HEADS, r, LANES), F32),
        compiler_params=_cparams(("arbitrary", "arbitrary")),
        name="bias_tables",
    )(rel_bias, buckets)


def _t5_bucket_np(dist):
    n = np.maximum(dist, 0)
    max_exact = N_BUCKETS // 2
    large = max_exact + (np.log(np.maximum(n, 1).astype(np.float32) / np.float32(max_exact))
                         / np.float32(math.log(MAX_DISTANCE / max_exact))
                         * np.float32(N_BUCKETS - max_exact)).astype(np.int32)
    large = np.minimum(large, N_BUCKETS - 1)
    return np.where(n < max_exact, n, large).astype(np.int32)


def _sortable(score):
    bits = pltpu.bitcast(score, I32)
    return jnp.where(bits < 0, bits ^ jnp.int32(0x7FFFFFFF), bits)


def _count_ge(key_ref, cand, ntiles, rows):
    candb = jnp.broadcast_to(cand, (rows, LANES))

    def body(t, acc):
        k = key_ref[:, pl.ds(pl.multiple_of(t * LANES, LANES), LANES)]
        return acc + jnp.where(k >= candb, 1.0, 0.0)

    acc = lax.fori_loop(0, ntiles, body, jnp.zeros((rows, LANES), F32))
    return jnp.sum(acc, axis=1, keepdims=True)


def _kth_largest_threshold(key_ref, ntiles, rows, kth):
    kf = float(kth)

    def bit_step(it, t):
        cand = t + jnp.left_shift(jnp.int32(1), 31 - it)
        cnt = _count_ge(key_ref, cand, ntiles, rows)
        return jnp.where(cnt >= kf, cand, t)

    t = lax.fori_loop(0, 32, bit_step, jnp.full((rows, 1), INT_MIN, I32))
    cnt_ge = _count_ge(key_ref, t, ntiles, rows)

    @pl.when(jnp.max(cnt_ge) > kf)
    def _():
        cnt_gt = _count_ge(key_ref, t + 1, ntiles, rows)
        r = kf - cnt_gt
        tb = jnp.broadcast_to(t, (rows, LANES))
        lane = lax.broadcasted_iota(I32, (rows, LANES), 1)

        def count_tie_below(y):
            yb = jnp.broadcast_to(y, (rows, LANES))

            def body(i, acc):
                k = key_ref[:, pl.ds(pl.multiple_of(i * LANES, LANES), LANES)]
                hit = jnp.where(k == tb, jnp.where(lane + i * LANES < yb, 1.0, 0.0), 0.0)
                return acc + hit

            acc = lax.fori_loop(0, ntiles, body, jnp.zeros((rows, LANES), F32))
            return jnp.sum(acc, axis=1, keepdims=True)

        def col_step(it, y):
            cand = y + jnp.left_shift(jnp.int32(1), 15 - it)
            return jnp.where(count_tie_below(cand) < r, cand, y)

        y = lax.fori_loop(0, 16, col_step, jnp.zeros((rows, 1), I32))
        yb = jnp.broadcast_to(y, (rows, LANES))

        def demote(i, carry):
            sl = pl.ds(pl.multiple_of(i * LANES, LANES), LANES)
            k = key_ref[:, sl]
            drop = jnp.where(k == tb, jnp.where(lane + i * LANES > yb, 1, 0), 0)
            key_ref[:, sl] = k - drop
            return carry

        lax.fori_loop(0, ntiles, demote, 0)

    return t


def _dsa_prompt_kernel(q_ref, k_ref, v_ref, qi_ref, ki_ref, wi_ref, bias_ref, y_ref,
                       kbf, vbf, kibf, keys, qs, qis, m_s, l_s, acc_s, *, topk):
    i = pl.program_id(1)
    qb = Q_BLOCK

    @pl.when(i == 0)
    def _():
        kbf[...] = k_ref[...].astype(BF16)
        vbf[...] = v_ref[...].astype(BF16)
        kibf[...] = ki_ref[:, 0:IDX_DIM].astype(BF16)

    for h in range(IDX_HEADS):
        qis[h * qb:(h + 1) * qb, :] = qi_ref[:, h * IDX_DIM:(h + 1) * IDX_DIM].astype(BF16)
    wsc = wi_ref[...] * (IDX_DIM ** -0.5 * IDX_HEADS ** -0.5)
    rowq = lax.broadcasted_iota(I32, (qb, LANES), 0)
    colk = lax.broadcasted_iota(I32, (qb, LANES), 1)

    def score_tile(t, carry):
        kt = kibf[pl.ds(pl.multiple_of(t * LANES, LANES), LANES), :]
        s = lax.dot_general(qis[...], kt, (((1,), (1,)), ((), ())), preferred_element_type=F32)
        sc = jnp.zeros((qb, LANES), F32)
        for h in range(IDX_HEADS):
            sc = sc + jnp.maximum(s[h * qb:(h + 1) * qb, :], 0.0) * wsc[:, h:h + 1]
        sc = jnp.where(colk + (t - i) * LANES <= rowq, sc, -jnp.inf)
        keys[:, pl.ds(pl.multiple_of(t * LANES, LANES), LANES)] = _sortable(sc)
        return carry

    lax.fori_loop(0, i + 1, score_tile, 0)

    tsel = jnp.full((qb, 1), KEY_NEG_INF + 1, I32)

    def search():
        return jnp.maximum(_kth_largest_threshold(keys, i + 1, qb, topk), tsel)

    t_found = lax.cond((i + 1) * qb > topk, search, lambda: tsel)
    tb = jnp.broadcast_to(t_found, (qb, LANES))

    for g in range(ATT_KV_HEADS):
        for j in range(ATT_GROUP):
            hq = g * ATT_GROUP + j
            qs[g, j * qb:(j + 1) * qb, :] = q_ref[:, hq * ATT_HEAD_DIM:(hq + 1) * ATT_HEAD_DIM].astype(BF16)
    m_s[...] = jnp.full(m_s.shape, NEG_BIG, F32)
    l_s[...] = jnp.zeros(l_s.shape, F32)
    acc_s[...] = jnp.zeros(acc_s.shape, F32)
    scale = ATT_HEAD_DIM ** -0.5

    def attend_tile(t, bias_idx):
        sl = pl.ds(pl.multiple_of(t * LANES, LANES), LANES)
        sel = keys[:, sl] >= tb
        kt = kbf[sl, :]
        vt = vbf[sl, :]
        for g in range(ATT_KV_HEADS):
            lg = lax.dot_general(qs[g], kt[:, g * ATT_HEAD_DIM:(g + 1) * ATT_HEAD_DIM],
                                 (((1,), (1,)), ((), ())), preferred_element_type=F32)
            lg = lg * scale + bias_ref[bias_idx, g]
            lg = jnp.concatenate([jnp.where(sel, lg[j * qb:(j + 1) * qb, :], NEG_BIG) for j in range(ATT_GROUP)],
                                 axis=0)
            m_old = m_s[g]
            m_new = jnp.maximum(m_old, jnp.max(lg, axis=1, keepdims=True))
            alpha = jnp.exp(m_old - m_new)
            p = jnp.exp(lg - m_new)
            l_s[g] = alpha * l_s[g] + jnp.sum(p, axis=1, keepdims=True)
            acc_s[g] = alpha * acc_s[g] + jnp.dot(p.astype(BF16), vt[:, g * ATT_HEAD_DIM:(g + 1) * ATT_HEAD_DIM],
                                                   preferred_element_type=F32)
            m_s[g] = m_new

    def far_tile(t, carry):
        attend_tile(t, 2)
        return carry

    lax.fori_loop(0, jnp.maximum(i - 1, 0), far_tile, 0)

    @pl.when(i >= 1)
    def _():
        attend_tile(i - 1, 1)

    attend_tile(i, 0)

    for g in range(ATT_KV_HEADS):
        o = acc_s[g] / l_s[g]
        for j in range(ATT_GROUP):
            hq = g * ATT_GROUP + j
            y_ref[:, hq * ATT_HEAD_DIM:(hq + 1) * ATT_HEAD_DIM] = o[j * qb:(j + 1) * qb, :].astype(y_ref.dtype)


def dsa_prompt(proj3, bias_tab):
    b, s, _ = proj3.shape
    nb = s // Q_BLOCK
    topk = min(TOPK_MAX, s // 4)
    qb = Q_BLOCK

    def qspec(name, width):
        blk = COL[name] // width
        return pl.BlockSpec((None, qb, width), lambda bi, i, blk=blk: (bi, i, blk))

    def kspec(name, width):
        blk = COL[name] // width
        return pl.BlockSpec((None, s, width), lambda bi, i, blk=blk: (bi, 0, blk))

    kvw = ATT_KV_HEADS * ATT_HEAD_DIM
    return pl.pallas_call(
        functools.partial(_dsa_prompt_kernel, topk=topk),
        grid=(b, nb),
        in_specs=[qspec("q", 1024), kspec("k", kvw), kspec("v", kvw), qspec("qi", 512), kspec("ki", LANES),
                  qspec("wi", LANES),
                  pl.BlockSpec(bias_tab.shape, lambda bi, i: (0, 0, 0, 0))],
        out_specs=pl.BlockSpec((None, qb, ATT_HEADS * ATT_HEAD_DIM), lambda bi, i: (bi, i, 0)),
        out_shape=jax.ShapeDtypeStruct((b, s, ATT_HEADS * ATT_HEAD_DIM), BF16),
        scratch_shapes=[pltpu.VMEM((s, kvw), BF16), pltpu.VMEM((s, kvw), BF16), pltpu.VMEM((s, IDX_DIM), BF16),
                        pltpu.VMEM((qb, s), I32),
                        pltpu.VMEM((ATT_KV_HEADS, ATT_GROUP * qb, ATT_HEAD_DIM), BF16),
                        pltpu.VMEM((IDX_HEADS * qb, IDX_DIM), BF16),
                        pltpu.VMEM((ATT_KV_HEADS, ATT_GROUP * qb, 1), F32),
                        pltpu.VMEM((ATT_KV_HEADS, ATT_GROUP * qb, 1), F32),
                        pltpu.VMEM((ATT_KV_HEADS, ATT_GROUP * qb, ATT_HEAD_DIM), F32)],
        compiler_params=_cparams(("arbitrary", "arbitrary")),
        name="dsa_prompt",
    )(proj3, proj3, proj3, proj3, proj3, proj3, bias_tab)


PAGES_PER_STEP = 8


def _page_specs(arr_ndim_tail, l, n_step):
    specs = []
    for r in range(PAGES_PER_STEP):
        specs.append(pl.BlockSpec((None, None) + arr_ndim_tail,
                                  lambda b, j, pt, r=r: (l, pt[b, j * PAGES_PER_STEP + r], 0, 0)))
    return specs


def _sample_scores_kernel(pt_ref, qi_ref, wi_ref, kin_ref, *rest):
    del pt_ref
    pages = rest[:PAGES_PER_STEP]
    o_ref, on_ref = rest[PAGES_PER_STEP:]
    j = pl.program_id(1)
    qi = qi_ref[...]
    w = wi_ref[...]
    n_new = qi.shape[0] // IDX_HEADS

    def scores(keys_page):
        s = jnp.maximum(_dot_nt(qi, keys_page), 0.0) * w
        return jnp.concatenate([jnp.sum(s[t * IDX_HEADS:(t + 1) * IDX_HEADS, :], axis=0, keepdims=True)
                                for t in range(n_new)], axis=0)

    for r in range(PAGES_PER_STEP):
        o_ref[:, r * LANES:(r + 1) * LANES] = scores(pages[r][...])

    @pl.when(j == 0)
    def _():
        sn = scores(kin_ref[...])
        rowt = lax.broadcasted_iota(I32, sn.shape, 0)
        colj = lax.broadcasted_iota(I32, sn.shape, 1)
        on_ref[...] = jnp.where(colj <= rowt, sn, -jnp.inf)


def sample_scores(l, page_table, qi_rows, wi_col, ki_new_pad, cache_kidx):
    b, n_pages = page_table.shape
    rows = qi_rows.shape[1]
    n_new = rows // IDX_HEADS
    n_step = n_pages // PAGES_PER_STEP
    seq = lambda shape: pl.BlockSpec((None,) + shape, lambda bi, j, pt: (bi,) + (0,) * len(shape))
    return pl.pallas_call(
        _sample_scores_kernel,
        grid_spec=pltpu.PrefetchScalarGridSpec(
            num_scalar_prefetch=1, grid=(b, n_step),
            in_specs=[seq((rows, IDX_DIM)), seq((rows, 1)), seq((PAGE_SIZE, IDX_DIM))]
            + _page_specs((PAGE_SIZE, IDX_DIM), l, n_step),
            out_specs=[pl.BlockSpec((None, n_new, PAGES_PER_STEP * PAGE_SIZE), lambda bi, j, pt: (bi, 0, j)),
                       pl.BlockSpec((None, n_new, LANES), lambda bi, j, pt: (bi, 0, 0))]),
        out_shape=[jax.ShapeDtypeStruct((b, n_new, n_pages * PAGE_SIZE), F32),
                   jax.ShapeDtypeStruct((b, n_new, LANES), F32)],
        compiler_params=_cparams(("arbitrary", "arbitrary")),
        name="sample_scores",
    )(page_table, qi_rows, wi_col, ki_new_pad, *([cache_kidx] * PAGES_PER_STEP))


def _threshold_kernel(s_ref, key_ref, t_ref, *, topk):
    rows, cols = s_ref.shape
    ntiles = cols // LANES

    def to_key(i, carry):
        sl = pl.ds(pl.multiple_of(i * LANES, LANES), LANES)
        key_ref[:, sl] = _sortable(s_ref[:, sl])
        return carry

    lax.fori_loop(0, ntiles, to_key, 0)
    t = _kth_largest_threshold(key_ref, ntiles, rows, topk)
    t_ref[...] = jnp.broadcast_to(t, (rows, LANES))


def topk_threshold(scores, topk):
    r, c = scores.shape
    return pl.pallas_call(
        functools.partial(_threshold_kernel, topk=topk),
        out_shape=[jax.ShapeDtypeStruct((r, c), I32), jax.ShapeDtypeStruct((r, LANES), I32)],
        compiler_params=pltpu.CompilerParams(vmem_limit_bytes=VMEM_LIMIT),
        name="topk_threshold",
    )(scores)


def _sample_attend_kernel(pt_ref, q_ref, key_ref, keyn_ref, t_ref, bias_ref, kn_ref, vn_ref, *rest,
                          n_step):
    del pt_ref
    kp = rest[:PAGES_PER_STEP]
    vp = rest[PAGES_PER_STEP:2 * PAGES_PER_STEP]
    y_ref, m_s, l_s, acc_s = rest[2 * PAGES_PER_STEP:]
    j = pl.program_id(1)
    rows = q_ref.shape[0]
    n_new = rows // ATT_HEADS
    scale = ATT_HEAD_DIM ** -0.5
    q = q_ref[...]
    tb = t_ref[...]
    rowh = lax.broadcasted_iota(I32, (rows, ATT_HEAD_DIM), 0) % ATT_HEADS

    @pl.when(j == 0)
    def _():
        m_s[...] = jnp.full(m_s.shape, NEG_BIG, F32)
        l_s[...] = jnp.zeros(l_s.shape, F32)
        acc_s[...] = jnp.zeros(acc_s.shape, F32)

    def expand_rows(x):
        return jnp.concatenate([jnp.broadcast_to(x[t:t + 1, :], (ATT_HEADS, LANES)) for t in range(n_new)], axis=0)

    def attend(kpage, vpage, keytile, bias):
        sel = expand_rows(keytile) >= expand_rows(tb)
        lg = _dot_nt(q, kpage) * scale + bias
        lg = jnp.where(sel, lg, NEG_BIG)
        m_old = m_s[...]
        m_new = jnp.maximum(m_old, jnp.max(lg, axis=1, keepdims=True))
        alpha = jnp.exp(m_old - m_new)
        p = jnp.exp(lg - m_new)
        l_s[...] = alpha * l_s[...] + jnp.sum(p, axis=1, keepdims=True)
        pv = _dot(p, vpage)
        pv = jnp.where(rowh < ATT_GROUP, pv[:, 0:ATT_HEAD_DIM], pv[:, ATT_HEAD_DIM:2 * ATT_HEAD_DIM])
        acc_s[...] = alpha * acc_s[...] + pv
        m_s[...] = m_new

    for r in range(PAGES_PER_STEP):
        keytile = key_ref[:, r * LANES:(r + 1) * LANES]
        if r == PAGES_PER_STEP - 1:
            bias = jnp.where(j == n_step - 1, bias_ref[0], bias_ref[2])
        else:
            bias = bias_ref[2]
        attend(kp[r][...], vp[r][...], keytile, bias)

    @pl.when(j == n_step - 1)
    def _():
        attend(kn_ref[...], vn_ref[...], keyn_ref[...], bias_ref[1])
        o = acc_s[...] / l_s[...]
        for t in range(n_new):
            for h in range(ATT_HEADS):
                y_ref[t:t + 1, h * ATT_HEAD_DIM:(h + 1) * ATT_HEAD_DIM] = (
                    o[t * ATT_HEADS + h:t * ATT_HEADS + h + 1, :].astype(y_ref.dtype))


def sample_attend(l, page_table, q_rows, keys_past, keys_new, thr, bias_tab, k_new_pad, v_new_pad, cache_k, cache_v):
    b, n_pages = page_table.shape
    rows = q_rows.shape[1]
    n_new = rows // ATT_HEADS
    n_step = n_pages // PAGES_PER_STEP
    kvw = ATT_KV_HEADS * ATT_HEAD_DIM
    seq = lambda shape: pl.BlockSpec((None,) + shape, lambda bi, j, pt: (bi,) + (0,) * len(shape))
    return pl.pallas_call(
        functools.partial(_sample_attend_kernel, n_step=n_step),
        grid_spec=pltpu.PrefetchScalarGridSpec(
            num_scalar_prefetch=1, grid=(b, n_step),
            in_specs=[seq((rows, kvw)),
                      pl.BlockSpec((None, n_new, PAGES_PER_STEP * PAGE_SIZE), lambda bi, j, pt: (bi, 0, j)),
                      seq((n_new, LANES)), seq((n_new, LANES)),
                      pl.BlockSpec(bias_tab.shape, lambda bi, j, pt: (0, 0, 0)),
                      seq((PAGE_SIZE, kvw)), seq((PAGE_SIZE, kvw))]
            + _page_specs((PAGE_SIZE, kvw), l, n_step) + _page_specs((PAGE_SIZE, kvw), l, n_step),
            out_specs=seq((n_new, ATT_HEADS * ATT_HEAD_DIM)),
            scratch_shapes=[pltpu.VMEM((rows, 1), F32), pltpu.VMEM((rows, 1), F32),
                            pltpu.VMEM((rows, ATT_HEAD_DIM), F32)]),
        out_shape=jax.ShapeDtypeStruct((b, n_new, ATT_HEADS * ATT_HEAD_DIM), BF16),
        compiler_params=_cparams(("arbitrary", "arbitrary")),
        name="sample_attend",
    )(page_table, q_rows, keys_past, keys_new, thr, bias_tab, k_new_pad, v_new_pad,
      *([cache_k] * PAGES_PER_STEP), *([cache_v] * PAGES_PER_STEP))


def _merge_kernel(lv_last, x_ref, ya_ref, yb_ref, gb_ref, gc_ref, hv_ref, ga_ref, gbb_ref, gcc_ref, g1_ref,
                  scprev_ref, scw_ref, wa_ref, wb_ref, wc_ref, wo_ref, o_ref, scnew_ref, ubuf):
    c = pl.program_id(1)
    nc = pl.num_programs(1)
    tm = x_ref.shape[0]

    @pl.when(c == 0)
    def _():
        ubuf[0:SUBLANES, :] = jnp.zeros((SUBLANES, D_MODEL), F32)
        ubuf[SUBLANES - 2:SUBLANES, :] = scprev_ref[...]

    ubuf[SUBLANES:SUBLANES + tm, :] = gc_ref[...] * hv_ref[...]
    conv = ubuf[pl.ds(SUBLANES - 2, tm), :] * scw_ref[0:1, :]
    for i in range(1, SC_KERNEL):
        conv = conv + ubuf[pl.ds(SUBLANES - 2 + i, tm), :] * scw_ref[i:i + 1, :]
    yc = gb_ref[...] * conv

    @pl.when(c == nc - 1)
    def _():
        scnew_ref[...] = ubuf[pl.ds(SUBLANES + lv_last - 2, 2), :]

    ubuf[0:SUBLANES, :] = ubuf[tm:tm + SUBLANES, :]

    sig = lambda r: 1.0 / (1.0 + jnp.exp(-r[...]))
    merged = (sig(ga_ref) * jnp.dot(ya_ref[...], wa_ref[...], preferred_element_type=F32)
              + sig(gbb_ref) * jnp.dot(yb_ref[...], wb_ref[...], preferred_element_type=F32)
              + sig(gcc_ref) * _dot(yc, wc_ref[...]))
    o_ref[...] = x_ref[...] + g1_ref[...] * _dot(merged, wo_ref[...])


def merge_branches(x3, ya, yb, proj3, g1, sc_prev, lw, tm, lv_last, per_row):
    b, l, _ = x3.shape
    nt = l // tm
    row = pl.BlockSpec((None, tm, D_MODEL), lambda i, c: (i, c, 0))

    def colspec(name, extra=0):
        blk = (COL[name] + extra) // D_MODEL
        return pl.BlockSpec((None, tm, D_MODEL), lambda i, c, blk=blk: (i, c, blk))

    g1spec = row if per_row else pl.BlockSpec((None, 1, D_MODEL), lambda i, c: (i, 0, 0))
    wspec = pl.BlockSpec((D_MODEL, D_MODEL), lambda i, c: (0, 0))
    return pl.pallas_call(
        functools.partial(_merge_kernel, lv_last),
        grid=(b, nt),
        in_specs=[row, row, row, colspec("gb"), colspec("gc"), colspec("hv"),
                  colspec("gates"), colspec("gates", 1024), colspec("gates", 2048), g1spec,
                  pl.BlockSpec((None, SC_KERNEL - 1, D_MODEL), lambda i, c: (i, 0, 0)),
                  pl.BlockSpec((SC_KERNEL, D_MODEL), lambda i, c: (0, 0)), wspec, wspec, wspec, wspec],
        out_specs=[row, pl.BlockSpec((None, SC_KERNEL - 1, D_MODEL), lambda i, c: (i, 0, 0))],
        out_shape=[jax.ShapeDtypeStruct((b, l, D_MODEL), F32),
                   jax.ShapeDtypeStruct((b, SC_KERNEL - 1, D_MODEL), F32)],
        scratch_shapes=[pltpu.VMEM((tm + SUBLANES, D_MODEL), F32)],
        compiler_params=_cparams(("arbitrary", "arbitrary")),
        name="merge_branches",
    )(x3, ya, yb, proj3, proj3, proj3, proj3, proj3, proj3, g1, sc_prev, lw["sc_conv_w"],
      lw["w_out_ssm"], lw["w_out_att"], lw["w_out_sc"], lw["w_o"])


EXPERTS_PER_STEP = 4


def _moe_kernel(final, x_ref, sc_ref, sh_ref, g2_ref, nw_ref, wr_ref, wg_ref, wu_ref, wd_ref, fw_ref, o_ref,
                h_s, comb_s, acc_s):
    e = pl.program_id(1)
    ne = pl.num_programs(1)
    tm = x_ref.shape[0]

    @pl.when(e == 0)
    def _():
        h = _modulated_norm(x_ref[...], nw_ref[...], sc_ref[...], sh_ref[...])
        hb = h.astype(BF16)
        h_s[...] = hb
        lg = jnp.dot(hb, wr_ref[...], preferred_element_type=F32)
        lane = lax.broadcasted_iota(I32, (tm, LANES), 1)
        is_g = lane < N_EXPERT_GROUPS
        gl = jnp.where(is_g, lg, -jnp.inf)
        gmax = jnp.max(gl, axis=1, keepdims=True)
        gexp = jnp.where(is_g, jnp.exp(gl - gmax), 0.0)
        pg = gexp / jnp.sum(gexp, axis=1, keepdims=True)
        pg_sel = jnp.max(pg, axis=1, keepdims=True)
        g_sel = jnp.min(jnp.where(is_g & (pg == pg_sel), lane, LANES), axis=1, keepdims=True)
        elane = lane - 32
        in_grp = (elane >= g_sel * EXPERTS_PER_GROUP) & (elane < (g_sel + 1) * EXPERTS_PER_GROUP)
        fl = jnp.where(in_grp, lg, -jnp.inf)
        fmax = jnp.max(fl, axis=1, keepdims=True)
        fexp = jnp.where(in_grp, jnp.exp(fl - fmax), 0.0)
        pf = fexp / jnp.sum(fexp, axis=1, keepdims=True)
        pf = jnp.where(in_grp, pf, -1.0)
        p1 = jnp.max(pf, axis=1, keepdims=True)
        i1 = jnp.min(jnp.where(pf == p1, lane, 2 * LANES), axis=1, keepdims=True)
        pf2 = jnp.where(lane == i1, -1.0, pf)
        p2 = jnp.max(pf2, axis=1, keepdims=True)
        i2 = jnp.min(jnp.where(pf2 == p2, lane, 2 * LANES), axis=1, keepdims=True)
        den = p1 + p2
        comb = jnp.where(lane == i1, pg_sel * p1 / den, 0.0) + jnp.where(lane == i2, pg_sel * p2 / den, 0.0)
        comb_s[...] = comb
        acc_s[...] = jnp.zeros(acc_s.shape, F32)

    hb = h_s[...]
    comb = comb_s[...]
    lane = lax.broadcasted_iota(I32, (tm, LANES), 1)
    acc = acc_s[...]
    for r in range(EXPERTS_PER_STEP):
        eid = e * EXPERTS_PER_STEP + r
        w = jnp.sum(jnp.where(lane == eid + 32, comb, 0.0), axis=1, keepdims=True)
        gate = jnp.dot(hb, wg_ref[r], preferred_element_type=F32)
        up = jnp.dot(hb, wu_ref[r], preferred_element_type=F32)
        act = _silu(gate) * up * w
        acc = acc + jnp.dot(act.astype(BF16), wd_ref[r], preferred_element_type=F32)
    acc_s[...] = acc

    @pl.when(e == ne - 1)
    def _():
        xo = x_ref[...] + g2_ref[...] * acc
        if final:
            ms = jnp.mean(xo * xo, axis=-1, keepdims=True)
            xo = xo * lax.rsqrt(ms + EPS) * fw_ref[...]
        o_ref[...] = xo


def hier_moe_block(x2d, sc, sh, g2, lw, final_w, tm, per_row, rows_per_seq, final):
    t = x2d.shape[0]
    ms = _mod_spec(per_row, tm, rows_per_seq)
    es = EXPERTS_PER_STEP
    vec = pl.BlockSpec((1, D_MODEL), lambda i, e: (0, 0))
    return pl.pallas_call(
        functools.partial(_moe_kernel, final),
        grid=(t // tm, N_EXPERTS // es),
        in_specs=[pl.BlockSpec((tm, D_MODEL), lambda i, e: (i, 0)), ms, ms, ms, vec,
                  pl.BlockSpec((D_MODEL, LANES), lambda i, e: (0, 0)),
                  pl.BlockSpec((es, D_MODEL, EXPERT_FF), lambda i, e: (e, 0, 0)),
                  pl.BlockSpec((es, D_MODEL, EXPERT_FF), lambda i, e: (e, 0, 0)),
                  pl.BlockSpec((es, EXPERT_FF, D_MODEL), lambda i, e: (e, 0, 0)), vec],
        out_specs=pl.BlockSpec((tm, D_MODEL), lambda i, e: (i, 0)),
        out_shape=jax.ShapeDtypeStruct((t, D_MODEL), F32),
        scratch_shapes=[pltpu.VMEM((tm, D_MODEL), BF16), pltpu.VMEM((tm, LANES), F32), pltpu.VMEM((tm, D_MODEL), F32)],
        compiler_params=_cparams(("arbitrary", "arbitrary")),
        name="hier_moe",
    )(x2d, sc, sh, g2, lw["norm2_w"], lw["w_router"], lw["exp_w_gate"], lw["exp_w_up"], lw["exp_w_down"], final_w)


def _prep_in_weight(w_in):
    parts = []
    for n in _ORDER:
        s, w = _ORIG[n]
        p = w_in[:, :, s:s + w]
        pad = -w % LANES
        if pad:
            p = jnp.pad(p, ((0, 0), (0, 0), (0, pad)))
        parts.append(p)
    return jnp.concatenate(parts, axis=-1).astype(BF16)


def _pad_lanes(a):
    return jnp.pad(a, ((0, 0), (0, LANES - a.shape[-1])))


def kernel(x_prompt, x_sample, c_prompt, c_sample, cache_k, cache_v, cache_kidx, state_ssm, state_ssm_conv, state_sc_conv, page_table, ada_w, ada_b, norm1_w, norm2_w, w_in, ssm_conv_w, ssm_conv_b, ssm_dt_bias, ssm_a_log, ssm_d, ssm_norm_w, sc_conv_w, rel_bias, w_out_ssm, w_out_att, w_out_sc, w_o, router_group_w, router_expert_w, exp_w_gate, exp_w_up, exp_w_down, final_norm_w):
    bp, seq, _ = x_prompt.shape
    bs, n_new, _ = x_sample.shape
    n_pool = cache_k.shape[1]
    n_pages = page_table.shape[1]
    past = n_pages * PAGE_SIZE
    kvw = ATT_KV_HEADS * ATT_HEAD_DIM
    ts = bs * n_new

    w_in_bf = _prep_in_weight(w_in)
    w_router = jnp.concatenate(
        [router_group_w, jnp.zeros((DEPTH, D_MODEL, 32 - N_EXPERT_GROUPS), F32), router_expert_w,
         jnp.zeros((DEPTH, D_MODEL, LANES - 32 - N_EXPERTS), F32)], axis=-1).astype(BF16)
    wg_bf, wu_bf, wd_bf = exp_w_gate.astype(BF16), exp_w_up.astype(BF16), exp_w_down.astype(BF16)
    wa_bf, wb_bf, wc_bf, wo_bf = (w.astype(BF16) for w in (w_out_ssm, w_out_att, w_out_sc, w_o))
    dtb = _pad_lanes(ssm_dt_bias)
    alog = _pad_lanes(ssm_a_log)
    dskip = jnp.repeat(ssm_d, SSD_HEAD_DIM, axis=-1)
    final_w = final_norm_w.reshape(1, D_MODEL)
    cache_k4 = cache_k.reshape(DEPTH, n_pool, PAGE_SIZE, kvw)
    cache_v4 = cache_v.reshape(DEPTH, n_pool, PAGE_SIZE, kvw)

    n_seq = bp + bs
    c_all = jnp.pad(jnp.concatenate([c_prompt, c_sample], axis=0), ((0, -n_seq % SUBLANES), (0, 0)))
    mod = ada_modulation_all(c_all, ada_w, ada_b)

    qi_ = np.arange(Q_BLOCK)[:, None]
    kj_ = np.arange(LANES)[None, :]
    bk_prompt = np.stack([_t5_bucket_np(qi_ - kj_), _t5_bucket_np(Q_BLOCK + qi_ - kj_),
                          np.full((Q_BLOCK, LANES), N_BUCKETS - 1, np.int32)])
    bt_prompt = bias_tables(rel_bias, jnp.asarray(bk_prompt))
    bt_prompt = bt_prompt.reshape(3, ATT_KV_HEADS, ATT_GROUP * Q_BLOCK, LANES)
    t_ = np.arange(SUBLANES)[:, None]
    bk_sample = np.stack([_t5_bucket_np(past + t_ - (past - PAGE_SIZE + kj_)), _t5_bucket_np(t_ - kj_),
                          np.full((SUBLANES, LANES), N_BUCKETS - 1, np.int32)])
    bt_sample = bias_tables(rel_bias, jnp.asarray(bk_sample))
    bt_sample = bt_sample[:, :, :n_new, :].transpose(0, 2, 1, 3).reshape(3, n_new * ATT_HEADS, LANES)

    xp = x_prompt
    xs3 = x_sample
    outs_p = [[] for _ in range(6)]
    outs_s = [[] for _ in range(6)]
    topk_s = min(TOPK_MAX, (past + n_new) // 4)
    for l in range(DEPTH):
        lw = dict(ssm_conv_w=ssm_conv_w[l], ssm_conv_b=ssm_conv_b[l].reshape(1, -1), dt_bias=dtb[l:l + 1],
                  a_log=alog[l:l + 1], d_skip=dskip[l:l + 1], ssm_norm_w=ssm_norm_w[l].reshape(1, -1),
                  sc_conv_w=sc_conv_w[l], w_out_ssm=wa_bf[l], w_out_att=wb_bf[l], w_out_sc=wc_bf[l], w_o=wo_bf[l],
                  norm2_w=norm2_w[l].reshape(1, -1), w_router=w_router[l], exp_w_gate=wg_bf[l], exp_w_up=wu_bf[l],
                  exp_w_down=wd_bf[l])
        m = mod[l]
        mp = [m[:bp, i * D_MODEL:(i + 1) * D_MODEL].reshape(bp, 1, D_MODEL) for i in range(6)]
        msr = [jnp.repeat(m[bp:n_seq, i * D_MODEL:(i + 1) * D_MODEL], n_new, axis=0) for i in range(6)]
        nw1 = norm1_w[l].reshape(1, D_MODEL)
        last = l == DEPTH - 1

        proj = in_projection(xp.reshape(bp * seq, D_MODEL), mp[1], mp[0], nw1, w_in_bf[l], min(1024, seq), False, seq)
        proj3 = proj.reshape(bp, seq, NP_IN)
        ya, ssm_h, ssm_conv_new = ssd_mixer(proj3, jnp.zeros((bp, SSD_CONV - 1, SSD_CONV_DIM), F32),
                                            jnp.zeros((bp, SSD_INNER, SSD_STATE), F32), lw, SSD_CHUNK)
        yb = dsa_prompt(proj3, bt_prompt)
        x1, sc_conv_new = merge_branches(xp, ya, yb, proj3, mp[2], jnp.zeros((bp, SC_KERNEL - 1, D_MODEL), F32), lw,
                                         512, 512, False)
        xp = hier_moe_block(x1.reshape(bp * seq, D_MODEL), mp[4], mp[3], mp[5], lw, final_w, 512, False, seq,
                            last).reshape(bp, seq, D_MODEL)
        outs_p[0].append(proj3[:, :, COL["k"]:COL["k"] + kvw].reshape(bp, seq, ATT_KV_HEADS, ATT_HEAD_DIM))
        outs_p[1].append(proj3[:, :, COL["v"]:COL["v"] + kvw].reshape(bp, seq, ATT_KV_HEADS, ATT_HEAD_DIM))
        outs_p[2].append(proj3[:, :, COL["ki"]:COL["ki"] + IDX_DIM])
        outs_p[3].append(ssm_h.reshape(bp, SSD_HEADS, SSD_HEAD_DIM, SSD_STATE))
        outs_p[4].append(ssm_conv_new)
        outs_p[5].append(sc_conv_new)

        projs = in_projection(xs3.reshape(ts, D_MODEL), msr[1], msr[0], nw1, w_in_bf[l], ts, True, n_new)
        projs3 = projs.reshape(bs, n_new, NP_IN)
        projs_pad = jnp.pad(projs3, ((0, 0), (0, SSD_CHUNK - n_new), (0, 0)))
        ya_s, ssm_h_s, ssm_conv_new_s = ssd_mixer(projs_pad, state_ssm_conv[l],
                                                  state_ssm[l].reshape(bs, SSD_INNER, SSD_STATE), lw, n_new)
        k_new = projs3[:, :, COL["k"]:COL["k"] + kvw]
        v_new = projs3[:, :, COL["v"]:COL["v"] + kvw]
        ki_new = projs3[:, :, COL["ki"]:COL["ki"] + IDX_DIM]
        pad_page = lambda a: jnp.pad(a, ((0, 0), (0, PAGE_SIZE - n_new), (0, 0)))
        qi_rows = projs3[:, :, COL["qi"]:COL["qi"] + IDX_HEADS * IDX_DIM].reshape(bs, n_new * IDX_HEADS, IDX_DIM)
        wi_col = (projs3[:, :, COL["wi"]:COL["wi"] + IDX_HEADS] * (IDX_DIM ** -0.5 * IDX_HEADS ** -0.5)
                  ).reshape(bs, n_new * IDX_HEADS, 1)
        s_past, s_new = sample_scores(l, page_table, qi_rows, wi_col, pad_page(ki_new), cache_kidx)
        s_all = jnp.concatenate([s_past, s_new], axis=-1).reshape(ts, past + LANES)
        keys_all, thr = topk_threshold(s_all, topk_s)
        keys_all = keys_all.reshape(bs, n_new, past + LANES)
        qh = projs3[:, :, COL["q"]:COL["q"] + ATT_HEADS * ATT_HEAD_DIM].reshape(bs, n_new, ATT_KV_HEADS, ATT_GROUP,
                                                                                ATT_HEAD_DIM)
        eye = jnp.eye(ATT_KV_HEADS, dtype=F32)
        q_rows = (qh[:, :, :, :, None, :] * eye[None, None, :, None, :, None]).reshape(bs, n_new * ATT_HEADS, kvw)
        yb_s = sample_attend(l, page_table, q_rows, keys_all, keys_all[:, :, past:], thr.reshape(bs, n_new, LANES),
                             bt_sample, pad_page(k_new), pad_page(v_new), cache_k4, cache_v4)
        g1s = msr[2].reshape(bs, n_new, D_MODEL)
        pad8 = lambda a: jnp.pad(a, ((0, 0), (0, SUBLANES - n_new), (0, 0)))
        x1s, sc_conv_new_s = merge_branches(pad8(xs3), pad8(ya_s[:, :n_new]), pad8(yb_s), pad8(projs3), pad8(g1s),
                                            state_sc_conv[l], lw, SUBLANES, n_new, True)
        x1s = x1s[:, :n_new]
        xs3 = hier_moe_block(x1s.reshape(ts, D_MODEL), msr[4], msr[3], msr[5], lw, final_w, ts, True, n_new,
                             last).reshape(bs, n_new, D_MODEL)
        outs_s[0].append(k_new.reshape(bs, n_new, ATT_KV_HEADS, ATT_HEAD_DIM))
        outs_s[1].append(v_new.reshape(bs, n_new, ATT_KV_HEADS, ATT_HEAD_DIM))
        outs_s[2].append(ki_new)
        outs_s[3].append(ssm_h_s.reshape(bs, SSD_HEADS, SSD_HEAD_DIM, SSD_STATE))
        outs_s[4].append(ssm_conv_new_s)
        outs_s[5].append(sc_conv_new_s)

    stk = lambda rows: [jnp.stack(r) for r in rows]
    return tuple([xp, xs3] + stk(outs_p) + stk(outs_s))
```
